```python
import math
import jax, jax.numpy as jnp
from jax import lax
import numpy as np

D_MODEL = 1024
BATCH = 8
SEQ = 2048
DEPTH = 2
DEC_BATCH = 128
DEC_SEQ = 8
PAST_LEN = 16384
PAGE_SIZE = 128

EPS = 1e-6
GROUP_W = D_MODEL // 4
D_MIX = 4 * GROUP_W
SSM_HEADS = 4
SSM_HEAD_DIM = GROUP_W // SSM_HEADS
SSM_GROUPS = 2
SSM_STATE = 64
SSM_CONV = 4
SSM_CHUNK = 64
SSM_XBC = GROUP_W + 2 * SSM_GROUPS * SSM_STATE
WINDOW = 128
SWA_HEADS = 4
SWA_KV_HEADS = 2
SWA_HEAD_DIM = GROUP_W // SWA_HEADS
SWA_GROUP = SWA_HEADS // SWA_KV_HEADS
SWA_BLOCK = 128
N_BUCKETS = 32
MAX_DISTANCE = 128
HG_HEADS = 4
HG_DK = GROUP_W // HG_HEADS
HG_DV = GROUP_W // HG_HEADS
HG_CHUNK = 16
LRU_WIDTH = GROUP_W
LRU_BLOCKS = 4
LRU_BLOCK_W = LRU_WIDTH // LRU_BLOCKS
LRU_CONV = 4
LRU_C = 8.0
D_FF = 4 * D_MODEL

SPLIT_SIZES = [
    GROUP_W,
    SSM_XBC,
    SSM_HEADS,
    SWA_HEADS * SWA_HEAD_DIM,
    SWA_KV_HEADS * SWA_HEAD_DIM,
    SWA_KV_HEADS * SWA_HEAD_DIM,
    HG_HEADS * HG_DK,
    HG_HEADS * HG_DK,
    HG_HEADS * HG_DV,
    HG_HEADS * HG_DV,
    LRU_WIDTH,
    LRU_WIDTH,
]
N_IN = sum(SPLIT_SIZES)
SPLIT_POINTS = [int(v) for v in np.cumsum(SPLIT_SIZES)[:-1]]

kernel_name = 'hybrid_hymba_ssd_swa_hgrn2_rglru_step'


def rmsnorm(x, w):
    xf = x.astype(jnp.float32)
    y = xf * lax.rsqrt(jnp.mean(xf * xf, axis=-1, keepdims=True) + EPS)
    return (y * w.astype(jnp.float32)).astype(x.dtype)


def causal_dwconv(x, buf, w, b):
    K = w.shape[0]
    L = x.shape[1]
    xc = jnp.concatenate([buf.astype(x.dtype), x], axis=1)
    y = b.astype(x.dtype)
    for k in range(K):
        y = y + xc[:, k:k + L] * w[k].astype(x.dtype)
    return y, xc[:, -(K - 1):]


def pad_time(x, pad):
    if pad == 0:
        return x
    widths = [(0, 0)] * x.ndim
    widths[1] = (0, pad)
    return jnp.pad(x, widths)


def ssd_chunked(x, dt, A, Bm, Cm, h0, chunk):
    b, L, H, P = x.shape
    N = Bm.shape[-1]
    T = min(chunk, L)
    pad = (-L) % T
    x, dt, Bm, Cm = pad_time(x, pad), pad_time(dt, pad), pad_time(Bm, pad), pad_time(Cm, pad)
    nc = (L + pad) // T
    xs = (x * dt[..., None]).reshape(b, nc, T, H, P)
    a = (dt * A).reshape(b, nc, T, H)
    Bc = Bm.reshape(b, nc, T, H, N)
    Cc = Cm.reshape(b, nc, T, H, N)
    acum = jnp.cumsum(a, axis=2)
    seg = acum[:, :, :, None, :] - acum[:, :, None, :, :]
    mask = jnp.tril(jnp.ones((T, T), dtype=bool))
    decay = jnp.exp(jnp.where(mask[:, :, None], seg, -jnp.inf))
    scores = jnp.einsum('bcthn,bcshn->bctsh', Cc, Bc) * decay
    y_diag = jnp.einsum('bctsh,bcshp->bcthp', scores, xs)
    decay_s = jnp.exp(acum[:, :, -1:, :] - acum)
    states = jnp.einsum('bcshn,bcsh,bcshp->bchpn', Bc, decay_s, xs)
    chunk_decay = jnp.exp(acum[:, :, -1, :])

    def step(S, inp):
        st, dec = inp
        return S * dec[:, :, None, None] + st, S

    S_fin, S_in = lax.scan(step, h0, (jnp.moveaxis(states, 1, 0), jnp.moveaxis(chunk_decay, 1, 0)))
    S_in = jnp.moveaxis(S_in, 0, 1)
    y_off = jnp.einsum('bcthn,bchpn,bcth->bcthp', Cc, S_in, jnp.exp(acum))
    y = (y_diag + y_off).reshape(b, nc * T, H, P)[:, :L]
    return y, S_fin


def ssd_mixer(z, xbc, dt_raw, conv_buf, h0, conv_w, conv_b, dt_bias, A_log, Dskip, norm_w):
    b, L, _ = z.shape
    xbc, new_buf = causal_dwconv(xbc, conv_buf, conv_w, conv_b)
    xbc = jax.nn.silu(xbc)
    xs, Bm, Cm = jnp.split(xbc, [GROUP_W, GROUP_W + SSM_GROUPS * SSM_STATE], axis=-1)
    xs = xs.reshape(b, L, SSM_HEADS, SSM_HEAD_DIM)
    rep = SSM_HEADS // SSM_GROUPS
    Bm = jnp.repeat(Bm.reshape(b, L, SSM_GROUPS, SSM_STATE), rep, axis=2)
    Cm = jnp.repeat(Cm.reshape(b, L, SSM_GROUPS, SSM_STATE), rep, axis=2)
    dt = jax.nn.softplus(dt_raw + dt_bias.astype(jnp.float32))
    A = -jnp.exp(A_log.astype(jnp.float32))
    y, S = ssd_chunked(xs, dt, A, Bm, Cm, h0.astype(jnp.float32), SSM_CHUNK)
    y = y + xs * Dskip.astype(jnp.float32)[:, None]
    y = rmsnorm(y.reshape(b, L, GROUP_W) * jax.nn.silu(z), norm_w)
    return y, S, new_buf


def t5_buckets(n):
    max_exact = N_BUCKETS // 2
    nf = np.maximum(n, 1).astype(np.float32)
    large = max_exact + (np.log(nf / max_exact) / math.log(MAX_DISTANCE / max_exact)
                         * (N_BUCKETS - max_exact)).astype(np.int32)
    large = np.minimum(large, N_BUCKETS - 1)
    return np.where(n < max_exact, n, large).astype(np.int32)


def swa_mixer(q, k, v, k_buf, v_buf, pos0, sinks, rel_bias):
    b, L = q.shape[:2]
    Bq = min(SWA_BLOCK, L)
    pad = (-L) % Bq
    Lp = L + pad
    nb = Lp // Bq
    S = Bq + WINDOW
    k_cat = jnp.concatenate([k_buf.astype(jnp.float32), k], axis=1)
    v_cat = jnp.concatenate([v_buf.astype(jnp.float32), v], axis=1)
    new_k = k_cat[:, -WINDOW:]
    new_v = v_cat[:, -WINDOW:]
    qp = pad_time(q, pad).reshape(b, nb, Bq, SWA_KV_HEADS, SWA_GROUP, SWA_HEAD_DIM)
    idx = np.arange(nb)[:, None] * Bq + np.arange(S)[None, :]
    kb = pad_time(k_cat, pad)[:, idx]
    vb = pad_time(v_cat, pad)[:, idx]
    logits = jnp.einsum('bntkgd,bnskd->bnkgts', qp, kb) * (SWA_HEAD_DIM ** -0.5)
    t = np.arange(Bq)
    s = np.arange(S)
    diff = t[:, None] - s[None, :] + WINDOW
    k_pos = pos0 - WINDOW + np.arange(nb)[:, None] * Bq + s[None, :]
    valid = ((diff >= 0) & (diff < WINDOW))[None] & (k_pos >= 0)[:, None, :]
    bias = rel_bias.astype(jnp.float32)[t5_buckets(np.clip(diff, 0, WINDOW - 1))]
    bias = jnp.transpose(bias, (2, 0, 1)).reshape(SWA_KV_HEADS, SWA_GROUP, Bq, S)
    logits = jnp.where(valid[None, :, None, None], logits + bias[None, None], -jnp.inf)
    sink = jnp.broadcast_to(sinks.astype(jnp.float32).reshape(SWA_KV_HEADS, SWA_GROUP)[None, None, :, :, None, None],
                            logits.shape[:-1] + (1,))
    p = jax.nn.softmax(jnp.concatenate([logits, sink], axis=-1), axis=-1)[..., :-1]
    o = jnp.einsum('bnkgts,bnskd->bntkgd', p, vb).reshape(b, Lp, SWA_HEADS * SWA_HEAD_DIM)[:, :L]
    return o, new_k, new_v


def hgrn_lower_bounds(logits):
    p = jax.nn.softmax(logits.astype(jnp.float32), axis=0)
    return jnp.maximum(jnp.cumsum(p, axis=0) - p[0:1], 0.0)


def gla_chunked(q, logf, kk, v, S0, chunk):
    b, L, H, K = q.shape
    V = v.shape[-1]
    T = min(chunk, L)
    pad = (-L) % T
    q, logf, kk, v = pad_time(q, pad), pad_time(logf, pad), pad_time(kk, pad), pad_time(v, pad)
    nc = (L + pad) // T
    q = q.reshape(b, nc, T, H, K)
    kk = kk.reshape(b, nc, T, H, K)
    v = v.reshape(b, nc, T, H, V)
    bc = jnp.cumsum(logf.reshape(b, nc, T, H, K), axis=2)
    mask = jnp.tril(jnp.ones((T, T), dtype=bool))
    diff = bc[:, :, :, None] - bc[:, :, None, :]
    D = jnp.exp(jnp.where(mask[:, :, None, None], diff, -jnp.inf))
    A = jnp.einsum('bcthk,bctshk,bcshk->bchts', q, D, kk)
    o_intra = jnp.einsum('bchts,bcshv->bcthv', A, v)
    blast = bc[:, :, -1]
    kdec = kk * jnp.exp(blast[:, :, None] - bc)
    U = jnp.einsum('bcshk,bcshv->bchkv', kdec, v)
    dec = jnp.exp(blast)

    def step(S, inp):
        u, d = inp
        return S * d[..., None] + u, S

    S_fin, S_in = lax.scan(step, S0, (jnp.moveaxis(U, 1, 0), jnp.moveaxis(dec, 1, 0)))
    S_in = jnp.moveaxis(S_in, 0, 1)
    o_inter = jnp.einsum('bcthk,bchkv->bcthv', q * jnp.exp(bc), S_in)
    o = (o_intra + o_inter).reshape(b, nc * T, H, V)[:, :L]
    return o, S_fin


def hgrn2_mixer(q, f_raw, v, g, S0, lb, norm_w):
    b, L = q.shape[:2]
    lb = lb.reshape(HG_HEADS, HG_DK)
    logf = jnp.logaddexp(jnp.log(lb), jnp.log1p(-lb) + jax.nn.log_sigmoid(f_raw))
    kk = (1.0 - lb) * jax.nn.sigmoid(-f_raw)
    o, S = gla_chunked(q, logf, kk, v, S0.astype(jnp.float32), HG_CHUNK)
    o = rmsnorm(o, norm_w.reshape(HG_HEADS, HG_DV)) * jax.nn.sigmoid(g)
    return o.reshape(b, L, HG_HEADS * HG_DV), S


def _lin_combine(e1, e2):
    a1, b1 = e1
    a2, b2 = e2
    return a1 * a2, a2 * b1 + b2


def rglru_mixer(xb, gate, conv_buf, h0, conv_w, conv_b, wa, ba, wx, bx, lam):
    xc, new_buf = causal_dwconv(xb, conv_buf, conv_w, conv_b)
    b, L, _ = xc.shape
    xblk = xc.reshape(b, L, LRU_BLOCKS, LRU_BLOCK_W)
    r = jax.nn.sigmoid(jnp.einsum('blni,nij->blnj', xblk, wa).reshape(b, L, LRU_WIDTH) + ba)
    i = jax.nn.sigmoid(jnp.einsum('blni,nij->blnj', xblk, wx).reshape(b, L, LRU_WIDTH) + bx)
    log_a = -LRU_C * r * jax.nn.softplus(-lam.astype(jnp.float32))
    a = jnp.exp(log_a)
    u = jnp.sqrt(-jnp.expm1(2.0 * log_a)) * (i * xc)
    a_cum, u_cum = lax.associative_scan(_lin_combine, (a, u), axis=1)
    h = a_cum * h0.astype(jnp.float32)[:, None] + u_cum
    y = h * jax.nn.gelu(gate, approximate=True)
    return y, h[:, -1], new_buf


def layer(x, pos0, li, ck, cv, s_ssm, s_ssmc, s_hg, s_lru, s_lruc, P, lb):
    dtype = x.dtype
    b, L = x.shape[:2]
    h = rmsnorm(x, P['norm_mix_w'][li])
    proj = jnp.einsum('bld,de->ble', h, P['w_in'][li]).astype(jnp.float32)
    (z, xbc, dt_raw, q_a, k_a, v_a, q_h, f_h, i_h, g_h, x_r, g_r) = jnp.split(proj, SPLIT_POINTS, axis=-1)
    y_ssd, n_ssm, n_ssmc = ssd_mixer(z, xbc, dt_raw, s_ssmc, s_ssm, P['ssm_conv_w'][li], P['ssm_conv_b'][li],
                                     P['ssm_dt_bias'][li], P['ssm_A_log'][li], P['ssm_D'][li], P['ssm_norm_w'][li])
    y_swa, n_k, n_v = swa_mixer(q_a.reshape(b, L, SWA_HEADS, SWA_HEAD_DIM),
                                k_a.reshape(b, L, SWA_KV_HEADS, SWA_HEAD_DIM),
                                v_a.reshape(b, L, SWA_KV_HEADS, SWA_HEAD_DIM),
                                ck, cv, pos0, P['swa_sinks'][li], P['rel_bias'])
    y_hg, n_hg = hgrn2_mixer(q_h.reshape(b, L, HG_HEADS, HG_DK), f_h.reshape(b, L, HG_HEADS, HG_DK),
                             i_h.reshape(b, L, HG_HEADS, HG_DV), g_h.reshape(b, L, HG_HEADS, HG_DV),
                             s_hg, lb, P['hgrn_norm_w'][li])
    y_lru, n_lru, n_lruc = rglru_mixer(x_r, g_r, s_lruc, s_lru, P['lru_conv_w'][li], P['lru_conv_b'][li],
                                       P['lru_wa'][li], P['lru_ba'][li], P['lru_wx'][li], P['lru_bx'][li],
                                       P['lru_lambda'][li])
    mix = jnp.concatenate([y_ssd, y_swa, y_hg, y_lru], axis=-1).astype(dtype)
    x = x + jnp.einsum('ble,ed->bld', mix, P['w_out'][li])
    h2 = rmsnorm(x, P['norm_mlp_w'][li])
    up = jnp.einsum('bld,df->blf', h2, P['w_up'][li])
    x = x + jnp.einsum('blf,fd->bld', jnp.square(jax.nn.relu(up)), P['w_down'][li])
    return x, (n_k, n_v, n_ssm, n_ssmc, n_hg, n_lru, n_lruc)


def trunk(x, pos0, c_k, c_v, s_ssm, s_ssmc, s_hg, s_lru, s_lruc, P):
    lb_all = hgrn_lower_bounds(P['hgrn_lb_logits'])
    outs = [[] for _ in range(7)]
    for li in range(DEPTH):
        x, new = layer(x, pos0, li, c_k[li], c_v[li], s_ssm[li], s_ssmc[li], s_hg[li], s_lru[li], s_lruc[li], P, lb_all[li])
        for o, n in zip(outs, new):
            o.append(n.astype(x.dtype))
    y = rmsnorm(x, P['norm_f_w'])
    return y, [jnp.stack(o, axis=0) for o in outs]


def setup_inputs(seed: int = 0) -> dict:
    key = jax.random.key(seed)
    ks = iter(jax.random.split(key, 48))

    def nrm(shape, s):
        return jax.random.normal(next(ks), shape, jnp.float32) * s

    def unif(shape, lo, hi):
        return jax.random.uniform(next(ks), shape, jnp.float32, lo, hi)

    x_prompt = nrm((BATCH, SEQ, D_MODEL), 1.0)
    x_sample = nrm((DEC_BATCH, DEC_SEQ, D_MODEL), 1.0)
    cache_swa_k = nrm((DEPTH, DEC_BATCH, WINDOW, SWA_KV_HEADS, SWA_HEAD_DIM), 1.0)
    cache_swa_v = nrm((DEPTH, DEC_BATCH, WINDOW, SWA_KV_HEADS, SWA_HEAD_DIM), 1.0)
    state_ssm = nrm((DEPTH, DEC_BATCH, SSM_HEADS, SSM_HEAD_DIM, SSM_STATE), 0.5)
    state_ssm_conv = nrm((DEPTH, DEC_BATCH, SSM_CONV - 1, SSM_XBC), 1.0)
    state_hgrn = nrm((DEPTH, DEC_BATCH, HG_HEADS, HG_DK, HG_DV), 0.5)
    state_lru = nrm((DEPTH, DEC_BATCH, LRU_WIDTH), 0.5)
    state_lru_conv = nrm((DEPTH, DEC_BATCH, LRU_CONV - 1, LRU_WIDTH), 1.0)
    norm_mix_w = 1.0 + nrm((DEPTH, D_MODEL), 0.02)
    w_in = nrm((DEPTH, D_MODEL, N_IN), D_MODEL ** -0.5)
    ssm_conv_w = nrm((DEPTH, SSM_CONV, SSM_XBC), SSM_CONV ** -0.5)
    ssm_conv_b = nrm((DEPTH, SSM_XBC), 0.01)
    dt0 = jnp.exp(unif((DEPTH, SSM_HEADS), math.log(1e-3), math.log(1e-1)))
    ssm_dt_bias = dt0 + jnp.log(-jnp.expm1(-dt0))
    ssm_A_log = jnp.log(unif((DEPTH, SSM_HEADS), 1.0, 16.0))
    ssm_D = 1.0 + nrm((DEPTH, SSM_HEADS), 0.01)
    ssm_norm_w = 1.0 + nrm((DEPTH, GROUP_W), 0.02)
    swa_sinks = nrm((DEPTH, SWA_HEADS), 1.0)
    rel_bias = nrm((N_BUCKETS, SWA_HEADS), 0.5)
    hgrn_lb_logits = nrm((DEPTH, HG_HEADS * HG_DK), 1.0)
    hgrn_norm_w = 1.0 + nrm((DEPTH, HG_HEADS * HG_DV), 0.02)
    lru_conv_w = nrm((DEPTH, LRU_CONV, LRU_WIDTH), LRU_CONV ** -0.5)
    lru_conv_b = nrm((DEPTH, LRU_WIDTH), 0.01)
    lru_wa = nrm((DEPTH, LRU_BLOCKS, LRU_BLOCK_W, LRU_BLOCK_W), LRU_BLOCK_W ** -0.5)
    lru_ba = nrm((DEPTH, LRU_WIDTH), 0.01)
    lru_wx = nrm((DEPTH, LRU_BLOCKS, LRU_BLOCK_W, LRU_BLOCK_W), LRU_BLOCK_W ** -0.5)
    lru_bx = nrm((DEPTH, LRU_WIDTH), 0.01)
    sig = unif((DEPTH, LRU_WIDTH), 0.9, 0.999) ** (1.0 / LRU_C)
    lru_lambda = jnp.log(sig) - jnp.log1p(-sig)
    w_out = nrm((DEPTH, D_MIX, D_MODEL), D_MIX ** -0.5)
    norm_mlp_w = 1.0 + nrm((DEPTH, D_MODEL), 0.02)
    w_up = nrm((DEPTH, D_MODEL, D_FF), D_MODEL ** -0.5)
    w_down = nrm((DEPTH, D_FF, D_MODEL), D_FF ** -0.5)
    norm_f_w = 1.0 + nrm((D_MODEL,), 0.02)
    return {
        'x_prompt': x_prompt, 'x_sample': x_sample,
        'cache_swa_k': cache_swa_k, 'cache_swa_v': cache_swa_v,
        'state_ssm': state_ssm, 'state_ssm_conv': state_ssm_conv, 'state_hgrn': state_hgrn,
        'state_lru': state_lru, 'state_lru_conv': state_lru_conv,
        'norm_mix_w': norm_mix_w, 'w_in': w_in,
        'ssm_conv_w': ssm_conv_w, 'ssm_conv_b': ssm_conv_b, 'ssm_dt_bias': ssm_dt_bias,
        'ssm_A_log': ssm_A_log, 'ssm_D': ssm_D, 'ssm_norm_w': ssm_norm_w,
        'swa_sinks': swa_sinks, 'rel_bias': rel_bias,
        'hgrn_lb_logits': hgrn_lb_logits, 'hgrn_norm_w': hgrn_norm_w,
        'lru_conv_w': lru_conv_w, 'lru_conv_b': lru_conv_b, 'lru_wa': lru_wa, 'lru_ba': lru_ba,
        'lru_wx': lru_wx, 'lru_bx': lru_bx, 'lru_lambda': lru_lambda,
        'w_out': w_out, 'norm_mlp_w': norm_mlp_w, 'w_up': w_up, 'w_down': w_down, 'norm_f_w': norm_f_w,
    }


def reference(x_prompt, x_sample, cache_swa_k, cache_swa_v, state_ssm, state_ssm_conv, state_hgrn,
              state_lru, state_lru_conv, norm_mix_w, w_in, ssm_conv_w, ssm_conv_b, ssm_dt_bias, ssm_A_log,
              ssm_D, ssm_norm_w, swa_sinks, rel_bias, hgrn_lb_logits, hgrn_norm_w, lru_conv_w, lru_conv_b,
              lru_wa, lru_ba, lru_wx, lru_bx, lru_lambda, w_out, norm_mlp_w, w_up, w_down, norm_f_w):
    P = dict(norm_mix_w=norm_mix_w, w_in=w_in, ssm_conv_w=ssm_conv_w, ssm_conv_b=ssm_conv_b,
             ssm_dt_bias=ssm_dt_bias, ssm_A_log=ssm_A_log, ssm_D=ssm_D, ssm_norm_w=ssm_norm_w,
             swa_sinks=swa_sinks, rel_bias=rel_bias, hgrn_lb_logits=hgrn_lb_logits, hgrn_norm_w=hgrn_norm_w,
             lru_conv_w=lru_conv_w, lru_conv_b=lru_conv_b, lru_wa=lru_wa, lru_ba=lru_ba, lru_wx=lru_wx,
             lru_bx=lru_bx, lru_lambda=lru_lambda, w_out=w_out, norm_mlp_w=norm_mlp_w, w_up=w_up,
             w_down=w_down, norm_f_w=norm_f_w)
    f32 = jnp.float32
    z_k = jnp.zeros((DEPTH, BATCH, WINDOW, SWA_KV_HEADS, SWA_HEAD_DIM), f32)
    z_ssm = jnp.zeros((DEPTH, BATCH, SSM_HEADS, SSM_HEAD_DIM, SSM_STATE), f32)
    z_ssmc = jnp.zeros((DEPTH, BATCH, SSM_CONV - 1, SSM_XBC), f32)
    z_hg = jnp.zeros((DEPTH, BATCH, HG_HEADS, HG_DK, HG_DV), f32)
    z_lru = jnp.zeros((DEPTH, BATCH, LRU_WIDTH), f32)
    z_lruc = jnp.zeros((DEPTH, BATCH, LRU_CONV - 1, LRU_WIDTH), f32)
    y_prompt, st_p = trunk(x_prompt, 0, z_k, z_k, z_ssm, z_ssmc, z_hg, z_lru, z_lruc, P)
    p_k, p_v, p_ssm, p_ssmc, p_hg, p_lru, p_lruc = st_p
    y_sample, st_s = trunk(x_sample, PAST_LEN, cache_swa_k, cache_swa_v, state_ssm, state_ssm_conv,
                           state_hgrn, state_lru, state_lru_conv, P)
    s_k, s_v, s_ssm, s_ssmc, s_hg, s_lru, s_lruc = st_s
    return (y_prompt, y_sample, p_k, p_v, p_ssm, p_ssmc, p_hg, p_lru, p_lruc,
            s_k, s_v, s_ssm, s_ssmc, s_hg, s_lru, s_lruc)
```

```python
import functools
import math

import numpy as np
import jax
import jax.numpy as jnp
from jax import lax
from jax.experimental import pallas as pl
from jax.experimental.pallas import tpu as pltpu

F32 = jnp.float32
BF16 = jnp.bfloat16
HIGHEST = lax.Precision.HIGHEST
NEG_INF = float("-inf")

D_MODEL = 1024
DEPTH = 2
EPS = 1e-6
GROUP_W = 256
HEADS = 4
HEAD_W = 64
SSM_XBC = 512
SSM_STATE = 64
CONV_K = 4
WINDOW = 128
N_BUCKETS = 32
MAX_DISTANCE = 128
LRU_C = 8.0
D_FF = 4096
DEC_SEQ = 8

TILE = 128
HG_CHUNK_PROMPT = 16
DENSE_TILE = 512
VMEM_LIMIT = 56 * 1024 * 1024

N_PROJ = 3072
SSD_COLS = 1024
SWA_COLS = 512
HG_COLS = 1024
LRU_COLS = 512


def _dot(a, b):
    return jnp.dot(a.astype(BF16), b.astype(BF16), preferred_element_type=F32)


def _dot_nt(a, b):
    return lax.dot_general(a.astype(BF16), b.astype(BF16), (((1,), (1,)), ((), ())),
                           preferred_element_type=F32)


def _dot_f32(a, b):
    return jnp.dot(a, b, precision=HIGHEST, preferred_element_type=F32)


def _sigmoid(x):
    return 1.0 / (1.0 + jnp.exp(-x))


def _softplus(x):
    return jnp.maximum(x, 0.0) + jnp.log1p(jnp.exp(-jnp.abs(x)))


def _iota(shape, dim):
    return lax.broadcasted_iota(jnp.int32, shape, dim)


def _causal_conv(x, prev, w_ref, b_ref, tmod):
    acc = b_ref[...] + x * w_ref[CONV_K - 1:CONV_K, :]
    for j in range(1, CONV_K):
        xj = jnp.where(tmod >= j, pltpu.roll(x, j, 0), pltpu.roll(prev, j, 0))
        acc = acc + xj * w_ref[CONV_K - 1 - j:CONV_K - j, :]
    return acc


def _full_spec(shape):
    nd = len(shape)
    return pl.BlockSpec(shape, lambda *_: (0,) * nd)


def _compiler_params(n_grid):
    return pltpu.CompilerParams(dimension_semantics=("arbitrary",) * n_grid, vmem_limit_bytes=VMEM_LIMIT)


def _inproj_body(x_ref, nw_ref, w_ref, ssd_ref, swa_ref, hg_ref, lru_ref):
    x = x_ref[...]
    h = x * lax.rsqrt(jnp.mean(x * x, axis=-1, keepdims=True) + EPS) * nw_ref[...]
    hb = h.astype(BF16)
    c0, c1, c2 = SSD_COLS, SSD_COLS + SWA_COLS, SSD_COLS + SWA_COLS + HG_COLS
    ssd_ref[...] = jnp.dot(hb, w_ref[:, 0:c0], preferred_element_type=F32)
    swa_ref[...] = jnp.dot(hb, w_ref[:, c0:c1], preferred_element_type=F32)
    hg_ref[...] = jnp.dot(hb, w_ref[:, c1:c2], preferred_element_type=F32)
    lru_ref[...] = jnp.dot(hb, w_ref[:, c2:N_PROJ], preferred_element_type=F32)


def _inproj(x2d, norm_w, w_p):
    n = x2d.shape[0]
    tm = min(DENSE_TILE, n)
    row = lambda c: pl.BlockSpec((tm, c), lambda i: (i, 0))
    return pl.pallas_call(
        _inproj_body,
        grid=(n // tm,),
        in_specs=[row(D_MODEL), _full_spec((1, D_MODEL)), _full_spec((D_MODEL, N_PROJ))],
        out_specs=[row(SSD_COLS), row(SWA_COLS), row(HG_COLS), row(LRU_COLS)],
        out_shape=[jax.ShapeDtypeStruct((n, c), F32) for c in (SSD_COLS, SWA_COLS, HG_COLS, LRU_COLS)],
        compiler_params=_compiler_params(1),
        name="inproj",
    )(x2d, norm_w, w_p)


def _mlp_body(final, x_ref, y0_ref, y1_ref, y2_ref, y3_ref, wo_ref, nw_ref, wu_ref, wd_ref, nf_ref, o_ref):
    mix = jnp.concatenate([y0_ref[...], y1_ref[...], y2_ref[...], y3_ref[...]], axis=-1)
    x1 = x_ref[...] + jnp.dot(mix.astype(BF16), wo_ref[...], preferred_element_type=F32)
    h2 = x1 * lax.rsqrt(jnp.mean(x1 * x1, axis=-1, keepdims=True) + EPS) * nw_ref[...]
    hb = h2.astype(BF16)
    acc = x1
    for c in range(D_FF // D_MODEL):
        cs = slice(c * D_MODEL, (c + 1) * D_MODEL)
        up = jnp.dot(hb, wu_ref[:, cs], preferred_element_type=F32)
        act = jnp.square(jnp.maximum(up, 0.0)).astype(BF16)
        acc = acc + jnp.dot(act, wd_ref[cs, :], preferred_element_type=F32)
    if final:
        acc = acc * lax.rsqrt(jnp.mean(acc * acc, axis=-1, keepdims=True) + EPS) * nf_ref[...]
    o_ref[...] = acc


def _mlp(x2d, ys, w_out, norm_w, w_up, w_down, norm_f, final):
    n = x2d.shape[0]
    tm = min(DENSE_TILE, n)
    row = lambda c: pl.BlockSpec((tm, c), lambda i: (i, 0))
    once = lambda shape: pl.BlockSpec(shape, lambda i: (0, 0), pipeline_mode=pl.Buffered(1))
    return pl.pallas_call(
        functools.partial(_mlp_body, final),
        grid=(n // tm,),
        in_specs=[row(D_MODEL)] + [row(GROUP_W)] * 4 + [
            once((D_MODEL, D_MODEL)), _full_spec((1, D_MODEL)), once((D_MODEL, D_FF)),
            once((D_FF, D_MODEL)), _full_spec((1, D_MODEL))],
        out_specs=row(D_MODEL),
        out_shape=jax.ShapeDtypeStruct((n, D_MODEL), F32),
        compiler_params=_compiler_params(1),
        name="outproj_mlp",
    )(x2d, *ys, w_out, norm_w, w_up, w_down, norm_f)


class _Geom:
    def __init__(self, nb, seq_rows):
        self.carry = seq_rows > TILE
        self.seq_len = TILE if self.carry else seq_rows
        self.nseq = TILE // self.seq_len
        self.nb = nb
        if self.carry:
            self.nt = seq_rows // TILE
            self.grid = (nb, self.nt)
            self.row_map = lambda b, t: (b * self.nt + t, 0)
            self.seq_map = lambda b, t: (b, 0, 0)
        else:
            self.grid = (nb // self.nseq,)
            self.row_map = lambda i: (i, 0)
            self.seq_map = lambda i: (i, 0, 0)

    def rows(self, cols):
        return pl.BlockSpec((TILE, cols), self.row_map)

    def per_seq(self, r, c):
        return pl.BlockSpec((self.nseq, r, c), self.seq_map)

    def first_tile(self):
        return pl.program_id(1) == 0

    def very_first_step(self):
        if self.carry:
            return (pl.program_id(0) == 0) & (pl.program_id(1) == 0)
        return pl.program_id(0) == 0


def _seq_masks(seq_len):
    r = np.arange(TILE)
    same = (r[:, None] // seq_len) == (r[None, :] // seq_len)
    lower = same & (r[None, :] <= r[:, None])
    return lower.astype(np.float32), same.astype(np.float32), np.where(lower, 0.0, NEG_INF).astype(np.float32)


def _lru_compute(seq_len, x_in, prev, h0rows, cw_ref, cb_ref, w_ref, bias_ref, lam_ref):
    xb = x_in[:, 0:GROUP_W]
    gate = x_in[:, GROUP_W:2 * GROUP_W]
    tmod = _iota((TILE, 1), 0) & (seq_len - 1)
    xc = _causal_conv(xb, prev, cw_ref, cb_ref, tmod)
    ri = _sigmoid(_dot(xc, w_ref[...]) + bias_ref[...])
    r = ri[:, 0:GROUP_W]
    i = ri[:, GROUP_W:2 * GROUP_W]
    log_a = -LRU_C * r * _softplus(-lam_ref[...])
    a = jnp.exp(log_a)
    u = jnp.sqrt(-jnp.tanh(log_a) * (a * a + 1.0)) * (i * xc)
    big_a, big_u = a, u
    d = 1
    while d < seq_len:
        ok = tmod >= d
        u_sh = pltpu.roll(big_u, d, 0)
        a_sh = pltpu.roll(big_a, d, 0)
        big_u = jnp.where(ok, big_a * u_sh + big_u, big_u)
        big_a = jnp.where(ok, big_a * a_sh, big_a)
        d *= 2
    h = big_a * h0rows + big_u
    gelu = 0.5 * gate * (1.0 + jnp.tanh(math.sqrt(2.0 / math.pi) * (gate + 0.044715 * (gate * gate * gate))))
    return h * gelu, h, xb


def _lru_prompt_body(geom, x_ref, cw_ref, cb_ref, w_ref, bias_ref, lam_ref, y_ref, hl_ref, prev_scr, h_scr):
    @pl.when(geom.first_tile())
    def _():
        prev_scr[...] = jnp.zeros_like(prev_scr)
        h_scr[...] = jnp.zeros_like(h_scr)

    y, h, xb = _lru_compute(geom.seq_len, x_ref[...], prev_scr[...], h_scr[...],
                            cw_ref, cb_ref, w_ref, bias_ref, lam_ref)
    y_ref[...] = y
    prev_scr[...] = xb
    h_scr[...] = h[TILE - 1:TILE, :]
    hl_ref[0] = h[TILE - 8:TILE, :]


def _lru_sample_body(geom, x_ref, prev_ref, h0_ref, cw_ref, cb_ref, w_ref, bias_ref, lam_ref, y_ref, h_ref):
    y, h, _ = _lru_compute(geom.seq_len, x_ref[...], prev_ref[...], h0_ref[...],
                           cw_ref, cb_ref, w_ref, bias_ref, lam_ref)
    y_ref[...] = y
    h_ref[...] = h


def _lru(geom, lru_in, params, prev=None, h0rows=None):
    n = lru_in.shape[0]
    p_specs = [_full_spec(p.shape) for p in params]
    if geom.carry:
        y, hl = pl.pallas_call(
            functools.partial(_lru_prompt_body, geom),
            grid=geom.grid,
            in_specs=[geom.rows(LRU_COLS)] + p_specs,
            out_specs=[geom.rows(GROUP_W), pl.BlockSpec((1, 8, GROUP_W), geom.seq_map)],
            out_shape=[jax.ShapeDtypeStruct((n, GROUP_W), F32),
                       jax.ShapeDtypeStruct((geom.nb, 8, GROUP_W), F32)],
            scratch_shapes=[pltpu.VMEM((TILE, GROUP_W), F32), pltpu.VMEM((1, GROUP_W), F32)],
            compiler_params=_compiler_params(2),
            name="lru_prompt",
        )(lru_in, *params)
        return y, hl[:, 7, :]
    y, h = pl.pallas_call(
        functools.partial(_lru_sample_body, geom),
        grid=geom.grid,
        in_specs=[geom.rows(LRU_COLS), geom.rows(GROUP_W), geom.rows(GROUP_W)] + p_specs,
        out_specs=[geom.rows(GROUP_W), geom.rows(GROUP_W)],
        out_shape=[jax.ShapeDtypeStruct((n, GROUP_W), F32)] * 2,
        compiler_params=_compiler_params(1),
        name="lru_sample",
    )(lru_in, prev, h0rows, *params)
    return y, h.reshape(geom.nb, geom.seq_len, GROUP_W)[:, -1, :]


def _ssd_compute(geom, x_in, prev, s_in, s_out, ltri_ref, ones_ref, neg_ref,
                 cw_ref, cb_ref, dtb_ref, alog_ref, dskip_ref, nw_ref):
    sl = geom.seq_len
    tmod = _iota((TILE, 1), 0) & (sl - 1)
    xbc = x_in[:, 0:SSM_XBC]
    z = x_in[:, SSM_XBC:SSM_XBC + GROUP_W]
    dtr = x_in[:, SSM_XBC + GROUP_W:SSD_COLS]
    xc = _causal_conv(xbc, prev, cw_ref, cb_ref, tmod)
    xc = xc * _sigmoid(xc)
    xs = xc[:, 0:GROUP_W]
    bm = xc[:, GROUP_W:GROUP_W + 128]
    cm = xc[:, GROUP_W + 128:SSM_XBC]
    dt = _softplus(dtr + dtb_ref[...])
    a = dt * (-jnp.exp(alog_ref[...]))
    acum = _dot_f32(ltri_ref[...], a)
    alast = _dot_f32(ones_ref[...], a)
    xsdt = xs * dt

    lane128 = _iota((1, 128), 1)
    lane256 = _iota((1, GROUP_W), 1)
    g_mats = [_dot_nt(jnp.where(lane128 < SSM_STATE, cm, 0.0), bm),
              _dot_nt(jnp.where(lane128 < SSM_STATE, 0.0, cm), bm)]
    acum_t = acum.T
    neg = neg_ref[...]
    y = xs * dskip_ref[...]
    for h in range(HEADS):
        col = jnp.sum(jnp.where(lane256 == HEAD_W * h, acum, 0.0), axis=-1, keepdims=True)
        row = acum_t[HEAD_W * h:HEAD_W * h + 1, :]
        decay = jnp.exp((col - row) + neg)
        scores = g_mats[h // 2] * decay
        in_head = (lane256 >= HEAD_W * h) & (lane256 < HEAD_W * (h + 1))
        y = y + _dot(scores, jnp.where(in_head, xsdt, 0.0))

    xw = xsdt * jnp.exp(alast - acum)
    bm_t = bm.T
    block = (_iota((128, GROUP_W), 0) // SSM_STATE) == (_iota((128, GROUP_W), 1) // 128)
    rowid = _iota((TILE, 1), 0)
    yoff = []
    for b in range(geom.nseq):
        r0 = b * sl
        s_b = s_in[b]
        yoff.append(_dot(cm[r0:r0 + sl, :], s_b))
        xw_b = xw if geom.nseq == 1 else jnp.where((rowid >= r0) & (rowid < r0 + sl), xw, 0.0)
        st = jnp.where(block, _dot(bm_t, xw_b), 0.0)
        s_out[b] = s_b * jnp.exp(alast[r0:r0 + 1, :]) + st
    yoff = yoff[0] if geom.nseq == 1 else jnp.concatenate(yoff, axis=0)
    y = y + yoff * jnp.exp(acum)
    yz = y * (z * _sigmoid(z))
    return yz * lax.rsqrt(jnp.mean(yz * yz, axis=-1, keepdims=True) + EPS) * nw_ref[...], xbc


def _ssd_prompt_body(geom, x_ref, *rest):
    consts_params = rest[:9]
    y_ref, s_ref, prev_scr, s_scr = rest[9:]

    @pl.when(geom.first_tile())
    def _():
        prev_scr[...] = jnp.zeros_like(prev_scr)
        s_scr[...] = jnp.zeros_like(s_scr)

    y, xbc = _ssd_compute(geom, x_ref[...], prev_scr[...], s_scr, s_scr, *consts_params)
    y_ref[...] = y
    prev_scr[...] = xbc
    s_ref[...] = s_scr[...]


def _ssd_sample_body(geom, x_ref, prev_ref, sin_ref, *rest):
    consts_params = rest[:9]
    y_ref, sout_ref = rest[9:]
    y, _ = _ssd_compute(geom, x_ref[...], prev_ref[...], sin_ref, sout_ref, *consts_params)
    y_ref[...] = y


def _ssd(geom, ssd_in, params, prev=None, s_in=None):
    n = ssd_in.shape[0]
    consts = [jnp.asarray(m) for m in _seq_masks(geom.seq_len)]
    cp = consts + list(params)
    cp_specs = [_full_spec(p.shape) for p in cp]
    s_shape = jax.ShapeDtypeStruct((geom.nb, 128, GROUP_W), F32)
    if geom.carry:
        return pl.pallas_call(
            functools.partial(_ssd_prompt_body, geom),
            grid=geom.grid,
            in_specs=[geom.rows(SSD_COLS)] + cp_specs,
            out_specs=[geom.rows(GROUP_W), geom.per_seq(128, GROUP_W)],
            out_shape=[jax.ShapeDtypeStruct((n, GROUP_W), F32), s_shape],
            scratch_shapes=[pltpu.VMEM((TILE, SSM_XBC), F32), pltpu.VMEM((1, 128, GROUP_W), F32)],
            compiler_params=_compiler_params(2),
            name="ssd_prompt",
        )(ssd_in, *cp)
    return pl.pallas_call(
        functools.partial(_ssd_sample_body, geom),
        grid=geom.grid,
        in_specs=[geom.rows(SSD_COLS), geom.rows(SSM_XBC), geom.per_seq(128, GROUP_W)] + cp_specs,
        out_specs=[geom.rows(GROUP_W), geom.per_seq(128, GROUP_W)],
        out_shape=[jax.ShapeDtypeStruct((n, GROUP_W), F32), s_shape],
        compiler_params=_compiler_params(1),
        name="ssd_sample",
    )(ssd_in, prev, s_in, *cp)


def _hgrn_compute(geom, chunk, x_in, s_in, s_out, ltri_ref, ones_ref, hsame_ref, lb_ref, nw_ref, p_scr, a_scr):
    nc = TILE // chunk
    q = x_in[:, 0:GROUP_W]
    f = x_in[:, GROUP_W:2 * GROUP_W]
    v = x_in[:, 2 * GROUP_W:3 * GROUP_W]
    g = x_in[:, 3 * GROUP_W:4 * GROUP_W]
    lb = lb_ref[...]
    log_sig = jnp.minimum(f, 0.0) - jnp.log1p(jnp.exp(-jnp.abs(f)))
    t_a = jnp.log(lb)
    t_b = jnp.log1p(-lb) + log_sig
    logf = jnp.maximum(t_a, t_b) + jnp.log1p(jnp.exp(-jnp.abs(t_a - t_b)))
    kk = (1.0 - lb) * _sigmoid(-f)
    bc = _dot_f32(ltri_ref[...], logf)
    blast = _dot_f32(ones_ref[...], logf)
    qe = q * jnp.exp(bc)
    kdec = kk * jnp.exp(blast - bc)

    rows_c = _iota((chunk, 1), 0)
    for c in range(nc):
        cs = slice(c * chunk, (c + 1) * chunk)
        bc_c, q_c, kk_c = bc[cs, :], q[cs, :], kk[cs, :]
        for s in range(chunk):
            e = jnp.exp(jnp.where(rows_c >= s, bc_c - bc_c[s:s + 1, :], NEG_INF))
            p_scr[(c * chunk + s) * chunk:(c * chunk + s + 1) * chunk, :] = q_c * e * kk_c[s:s + 1, :]
    a_scr[...] = jnp.dot(p_scr[...].astype(BF16), hsame_ref[...].astype(BF16), preferred_element_type=F32)
    o_intra = []
    for c in range(nc):
        acc = jnp.zeros((chunk, GROUP_W), F32)
        for s in range(chunk):
            r = c * chunk + s
            acc = acc + a_scr[r * chunk:(r + 1) * chunk, :] * v[r:r + 1, :]
        o_intra.append(acc)
    o_intra = jnp.concatenate(o_intra, axis=0)

    hsame = hsame_ref[...] > 0.5
    v_t = v.T
    cid = _iota((TILE, 1), 0) // chunk
    o_inter = []
    state = s_in[0]
    for c in range(nc):
        cs = slice(c * chunk, (c + 1) * chunk)
        if not geom.carry:
            state = s_in[c]
        o_inter.append(_dot_nt(qe[cs, :], state))
        upd = jnp.where(hsame, _dot(v_t, jnp.where(cid == c, kdec, 0.0)), 0.0)
        state = state * jnp.exp(blast[c * chunk:c * chunk + 1, :]) + upd
        if not geom.carry:
            s_out[c] = state
    if geom.carry:
        s_out[0] = state
    o = o_intra + jnp.concatenate(o_inter, axis=0)
    ms = _dot_f32(o * o, hsame_ref[...]) * (1.0 / HEAD_W)
    return o * lax.rsqrt(ms + EPS) * nw_ref[...] * _sigmoid(g)


def _hgrn_prompt_body(geom, chunk, x_ref, ltri_ref, ones_ref, hsame_ref, lb_ref, nw_ref, y_ref, s_ref,
                      s_scr, p_scr, a_scr):
    @pl.when(geom.first_tile())
    def _():
        s_scr[...] = jnp.zeros_like(s_scr)

    y_ref[...] = _hgrn_compute(geom, chunk, x_ref[...], s_scr, s_scr, ltri_ref, ones_ref, hsame_ref,
                               lb_ref, nw_ref, p_scr, a_scr)
    s_ref[...] = s_scr[...]


def _hgrn_sample_body(geom, chunk, x_ref, sin_ref, ltri_ref, ones_ref, hsame_ref, lb_ref, nw_ref, y_ref, sout_ref,
                      p_scr, a_scr):
    y_ref[...] = _hgrn_compute(geom, chunk, x_ref[...], sin_ref, sout_ref, ltri_ref, ones_ref, hsame_ref,
                               lb_ref, nw_ref, p_scr, a_scr)


def _hgrn(geom, hg_in, lb, norm_w, s_in=None):
    n = hg_in.shape[0]
    chunk = HG_CHUNK_PROMPT if geom.carry else geom.seq_len
    ltri, ones, _ = _seq_masks(chunk)
    lane = np.arange(GROUP_W)
    hsame = (lane[:, None] // HEAD_W == lane[None, :] // HEAD_W).astype(np.float32)
    cp = [jnp.asarray(ltri), jnp.asarray(ones), jnp.asarray(hsame), lb, norm_w]
    cp_specs = [_full_spec(p.shape) for p in cp]
    pa = [pltpu.VMEM((TILE * chunk, GROUP_W), F32)] * 2
    s_shape = jax.ShapeDtypeStruct((geom.nb, GROUP_W, GROUP_W), F32)
    if geom.carry:
        return pl.pallas_call(
            functools.partial(_hgrn_prompt_body, geom, chunk),
            grid=geom.grid,
            in_specs=[geom.rows(HG_COLS)] + cp_specs,
            out_specs=[geom.rows(GROUP_W), geom.per_seq(GROUP_W, GROUP_W)],
            out_shape=[jax.ShapeDtypeStruct((n, GROUP_W), F32), s_shape],
            scratch_shapes=[pltpu.VMEM((1, GROUP_W, GROUP_W), F32)] + pa,
            compiler_params=_compiler_params(2),
            name="hgrn_prompt",
        )(hg_in, *cp)
    return pl.pallas_call(
        functools.partial(_hgrn_sample_body, geom, chunk),
        grid=geom.grid,
        in_specs=[geom.rows(HG_COLS), geom.per_seq(GROUP_W, GROUP_W)] + cp_specs,
        out_specs=[geom.rows(GROUP_W), geom.per_seq(GROUP_W, GROUP_W)],
        out_shape=[jax.ShapeDtypeStruct((n, GROUP_W), F32), s_shape],
        scratch_shapes=pa,
        compiler_params=_compiler_params(1),
        name="hgrn_sample",
    )(hg_in, s_in, *cp)


def _t5_buckets(n):
    max_exact = N_BUCKETS // 2
    nf = np.maximum(n, 1).astype(np.float32)
    large = max_exact + (np.log(nf / max_exact) / math.log(MAX_DISTANCE / max_exact)
                         * (N_BUCKETS - max_exact)).astype(np.int32)
    large = np.minimum(large, N_BUCKETS - 1)
    return np.where(n < max_exact, n, large).astype(np.int32)


def _swa_tables(seq_len):
    r = np.arange(TILE)
    t = r % seq_len
    diff_prev = t[:, None] - np.arange(WINDOW)[None, :] + WINDOW
    ok_prev = (diff_prev >= 0) & (diff_prev < WINDOW)
    tbl_prev = np.where(ok_prev, _t5_buckets(np.clip(diff_prev, 0, WINDOW - 1)), -1)
    diff_cur = t[:, None] - t[None, :]
    ok_cur = (r[:, None] // seq_len == r[None, :] // seq_len) & (diff_cur >= 0) & (diff_cur < WINDOW)
    tbl_cur = np.where(ok_cur, _t5_buckets(np.clip(diff_cur, 0, WINDOW - 1)), -1)
    return tbl_prev.astype(np.int32), tbl_cur.astype(np.int32)


_SWA_HEAD_OF = ((0, 2), (1, 3))


def _swa_compute(geom, x_in, kprev, vprev, prev_neg, rb_ref, sink_ref, tblp_ref, tblc_ref, bp_scr, bc_scr):
    @pl.when(geom.very_first_step())
    def _():
        for tbl_ref, dst in ((tblp_ref, bp_scr), (tblc_ref, bc_scr)):
            tbl = tbl_ref[...]
            for hid in range(HEADS):
                bias = jnp.full(tbl.shape, NEG_INF, F32)
                for k in range(N_BUCKETS):
                    bias = jnp.where(tbl == k, rb_ref[k, hid], bias)
                dst[hid] = bias

    sl = geom.seq_len
    kc = x_in[:, 256:384]
    vc = x_in[:, 384:512]
    lane = _iota((1, 128), 1)
    outs = []
    for slot in range(2):
        q_slot = x_in[:, 128 * slot:128 * (slot + 1)]
        acc = jnp.zeros((TILE, 128), F32)
        for half in range(2):
            hid = _SWA_HEAD_OF[slot][half]
            in_half = (lane < HEAD_W) if half == 0 else (lane >= HEAD_W)
            qm = jnp.where(in_half, q_slot, 0.0) * (HEAD_W ** -0.5)
            lp = [_dot_nt(qm[b * sl:(b + 1) * sl, :], kprev[b]) for b in range(geom.nseq)]
            lp = lp[0] if geom.nseq == 1 else jnp.concatenate(lp, axis=0)
            lp = lp + bp_scr[hid] + prev_neg
            lc = _dot_nt(qm, kc) + bc_scr[hid]
            sink = sink_ref[hid]
            m = jnp.maximum(jnp.maximum(jnp.max(lp, axis=-1, keepdims=True),
                                        jnp.max(lc, axis=-1, keepdims=True)), sink)
            pp = jnp.exp(lp - m)
            pc = jnp.exp(lc - m)
            den = (jnp.sum(pp, axis=-1, keepdims=True) + jnp.sum(pc, axis=-1, keepdims=True)
                   + jnp.exp(sink - m))
            op = [_dot(pp[b * sl:(b + 1) * sl, :], vprev[b]) for b in range(geom.nseq)]
            op = op[0] if geom.nseq == 1 else jnp.concatenate(op, axis=0)
            o = (op + _dot(pc, vc)) / den
            acc = acc + jnp.where(in_half, o, 0.0)
        outs.append(acc)
    return jnp.concatenate(outs, axis=-1), kc, vc


def _swa_prompt_body(geom, x_ref, rb_ref, sink_ref, tblp_ref, tblc_ref, y_ref, kp_scr, vp_scr, bp_scr, bc_scr):
    @pl.when(geom.first_tile())
    def _():
        kp_scr[...] = jnp.zeros_like(kp_scr)
        vp_scr[...] = jnp.zeros_like(vp_scr)

    prev_neg = jnp.where(geom.first_tile(), NEG_INF, 0.0).astype(F32)
    y, kc, vc = _swa_compute(geom, x_ref[...], kp_scr, vp_scr, prev_neg, rb_ref, sink_ref,
                             tblp_ref, tblc_ref, bp_scr, bc_scr)
    y_ref[...] = y
    kp_scr[0] = kc
    vp_scr[0] = vc


def _swa_sample_body(geom, x_ref, kc_ref, vc_ref, rb_ref, sink_ref, tblp_ref, tblc_ref, y_ref, bp_scr, bc_scr):
    y, _, _ = _swa_compute(geom, x_ref[...], kc_ref, vc_ref, 0.0, rb_ref, sink_ref,
                           tblp_ref, tblc_ref, bp_scr, bc_scr)
    y_ref[...] = y


def _swa(geom, swa_in, rel_bias, sinks, k_cache=None, v_cache=None):
    n = swa_in.shape[0]
    tblp, tblc = _swa_tables(geom.seq_len)
    smem = pl.BlockSpec(memory_space=pltpu.SMEM)
    tail_specs = [smem, smem, _full_spec((TILE, WINDOW)), _full_spec((TILE, TILE))]
    tail = [rel_bias, sinks, jnp.asarray(tblp), jnp.asarray(tblc)]
    bias_scr = [pltpu.VMEM((HEADS, TILE, WINDOW), F32), pltpu.VMEM((HEADS, TILE, TILE), F32)]
    out_shape = jax.ShapeDtypeStruct((n, GROUP_W), F32)
    if geom.carry:
        return pl.pallas_call(
            functools.partial(_swa_prompt_body, geom),
            grid=geom.grid,
            in_specs=[geom.rows(SWA_COLS)] + tail_specs,
            out_specs=geom.rows(GROUP_W),
            out_shape=out_shape,
            scratch_shapes=[pltpu.VMEM((1, TILE, 128), F32)] * 2 + bias_scr,
            compiler_params=_compiler_params(2),
            name="swa_prompt",
        )(swa_in, *tail)
    return pl.pallas_call(
        functools.partial(_swa_sample_body, geom),
        grid=geom.grid,
        in_specs=[geom.rows(SWA_COLS), geom.per_seq(WINDOW, 128), geom.per_seq(WINDOW, 128)] + tail_specs,
        out_specs=geom.rows(GROUP_W),
        out_shape=out_shape,
        scratch_shapes=bias_scr,
        compiler_params=_compiler_params(1),
        name="swa_sample",
    )(swa_in, k_cache, v_cache, *tail)


def _pack_w_in(w):
    z, xbc, dt = w[:, 0:256], w[:, 256:768], w[:, 768:772]
    q = w[:, 772:1028]
    rest = w[:, 1028:]
    qh = [q[:, HEAD_W * h:HEAD_W * (h + 1)] for h in range(HEADS)]
    packed = jnp.concatenate([xbc, z, jnp.repeat(dt, HEAD_W, axis=1), qh[0], qh[2], qh[1], qh[3], rest], axis=1)
    return packed.astype(BF16)


def _pack_w_out(w):
    a = w[GROUP_W:2 * GROUP_W]
    ah = [a[HEAD_W * h:HEAD_W * (h + 1)] for h in range(HEADS)]
    return jnp.concatenate([w[0:GROUP_W], ah[0], ah[2], ah[1], ah[3], w[2 * GROUP_W:]], axis=0).astype(BF16)


def _block_diag4(w):
    return jnp.einsum('nij,nm->nimj', w, jnp.eye(HEADS, dtype=w.dtype)).reshape(GROUP_W, GROUP_W)


def _row(v):
    return v.reshape(1, -1).astype(F32)


def _hgrn_lower_bounds(logits):
    p = jax.nn.softmax(logits.astype(F32), axis=0)
    return jnp.maximum(jnp.cumsum(p, axis=0) - p[0:1], 0.0)


def _conv_history_rows(buf, nseq):
    b, _, c = buf.shape
    g = jnp.roll(buf.reshape(b // nseq, nseq, CONV_K - 1, c), -1, axis=1)
    g = jnp.pad(g, ((0, 0), (0, 0), (DEC_SEQ - (CONV_K - 1), 0), (0, 0)))
    return g.reshape(b * DEC_SEQ, c)


def _ssm_state_to_mat(s):
    b = s.shape[0]
    st = jnp.swapaxes(s, 2, 3).reshape(b, 2, 2, SSM_STATE, HEAD_W)
    m = jnp.einsum('bghnp,gk->bgnkhp', st, jnp.eye(2, dtype=s.dtype))
    return m.reshape(b, 128, GROUP_W)


def _ssm_mat_to_state(m):
    b = m.shape[0]
    m6 = m.reshape(b, 2, SSM_STATE, 2, 2, HEAD_W)
    d = jnp.stack([m6[:, 0, :, 0], m6[:, 1, :, 1]], axis=1)
    return jnp.transpose(d, (0, 1, 3, 4, 2)).reshape(b, HEADS, HEAD_W, SSM_STATE)


def _hg_state_to_mat(s):
    b = s.shape[0]
    return jnp.einsum('bhkv,hg->bhvgk', s, jnp.eye(HEADS, dtype=s.dtype)).reshape(b, GROUP_W, GROUP_W)


def _hg_mat_to_state(m):
    b = m.shape[0]
    m5 = m.reshape(b, HEADS, HEAD_W, HEADS, HEAD_W)
    d = jnp.stack([m5[:, h, :, h, :] for h in range(HEADS)], axis=1)
    return jnp.swapaxes(d, 2, 3)


def _trunk(x, P, sample_state):
    nb, seq, _ = x.shape
    geom = _Geom(nb, seq)
    x2d = x.reshape(nb * seq, D_MODEL)
    lb_all = _hgrn_lower_bounds(P['hgrn_lb_logits'])
    outs = [[] for _ in range(7)]
    for li in range(DEPTH):
        ssd_in, swa_in, hg_in, lru_in = _inproj(x2d, _row(P['norm_mix_w'][li]), P['w_in_p'][li])

        ssd_params = [P['ssm_conv_w'][li], _row(P['ssm_conv_b'][li]),
                      _row(jnp.repeat(P['ssm_dt_bias'][li], HEAD_W)), _row(jnp.repeat(P['ssm_A_log'][li], HEAD_W)),
                      _row(jnp.repeat(P['ssm_D'][li], HEAD_W)), _row(P['ssm_norm_w'][li])]
        lru_w = jnp.concatenate([_block_diag4(P['lru_wa'][li]), _block_diag4(P['lru_wx'][li])], axis=1).astype(BF16)
        lru_params = [P['lru_conv_w'][li], _row(P['lru_conv_b'][li]), lru_w,
                      _row(jnp.concatenate([P['lru_ba'][li], P['lru_bx'][li]])), _row(P['lru_lambda'][li])]
        if geom.carry:
            y_ssd, s_mat = _ssd(geom, ssd_in, ssd_params)
            y_swa = _swa(geom, swa_in, P['rel_bias'], P['swa_sinks'][li])
            y_hg, hg_mat = _hgrn(geom, hg_in, _row(lb_all[li]), _row(P['hgrn_norm_w'][li]))
            y_lru, h_last = _lru(geom, lru_in, lru_params)
            kv = swa_in.reshape(nb, seq, SWA_COLS)[:, seq - WINDOW:, :]
            n_k = kv[:, :, 256:384]
            n_v = kv[:, :, 384:512]
        else:
            c_k, c_v, s_ssm, s_ssmc, s_hg, s_lru, s_lruc = [s[li] for s in sample_state]
            y_ssd, s_mat = _ssd(geom, ssd_in, ssd_params, _conv_history_rows(s_ssmc, geom.nseq),
                                _ssm_state_to_mat(s_ssm))
            c_k2 = c_k.reshape(nb, WINDOW, 128)
            c_v2 = c_v.reshape(nb, WINDOW, 128)
            y_swa = _swa(geom, swa_in, P['rel_bias'], P['swa_sinks'][li], c_k2, c_v2)
            y_hg, hg_mat = _hgrn(geom, hg_in, _row(lb_all[li]), _row(P['hgrn_norm_w'][li]), _hg_state_to_mat(s_hg))
            y_lru, h_last = _lru(geom, lru_in, lru_params, _conv_history_rows(s_lruc, geom.nseq),
                                 jnp.repeat(s_lru, seq, axis=0))
            kv = swa_in.reshape(nb, seq, SWA_COLS)
            n_k = jnp.concatenate([c_k2[:, seq:, :], kv[:, :, 256:384]], axis=1)
            n_v = jnp.concatenate([c_v2[:, seq:, :], kv[:, :, 384:512]], axis=1)

        new = (n_k.reshape(nb, WINDOW, 2, HEAD_W), n_v.reshape(nb, WINDOW, 2, HEAD_W),
               _ssm_mat_to_state(s_mat),
               ssd_in.reshape(nb, seq, SSD_COLS)[:, seq - (CONV_K - 1):, 0:SSM_XBC],
               _hg_mat_to_state(hg_mat),
               h_last,
               lru_in.reshape(nb, seq, LRU_COLS)[:, seq - (CONV_K - 1):, 0:GROUP_W])
        for o, s in zip(outs, new):
            o.append(s)

        x2d = _mlp(x2d, (y_ssd, y_swa, y_hg, y_lru), P['w_out_p'][li], _row(P['norm_mlp_w'][li]),
                   P['w_up_b'][li], P['w_down_b'][li], _row(P['norm_f_w']), final=(li == DEPTH - 1))
    return x2d.reshape(nb, seq, D_MODEL), [jnp.stack(o, axis=0) for o in outs]


def kernel(x_prompt, x_sample, cache_swa_k, cache_swa_v, state_ssm, state_ssm_conv, state_hgrn, state_lru, state_lru_conv, norm_mix_w, w_in, ssm_conv_w, ssm_conv_b, ssm_dt_bias, ssm_A_log, ssm_D, ssm_norm_w, swa_sinks, rel_bias, hgrn_lb_logits, hgrn_norm_w, lru_conv_w, lru_conv_b, lru_wa, lru_ba, lru_wx, lru_bx, lru_lambda, w_out, norm_mlp_w, w_up, w_down, norm_f_w):
    P = dict(norm_mix_w=norm_mix_w, ssm_conv_w=ssm_conv_w, ssm_conv_b=ssm_conv_b,
             ssm_dt_bias=ssm_dt_bias, ssm_A_log=ssm_A_log, ssm_D=ssm_D, ssm_norm_w=ssm_norm_w,
             swa_sinks=swa_sinks, rel_bias=rel_bias, hgrn_lb_logits=hgrn_lb_logits, hgrn_norm_w=hgrn_norm_w,
             lru_conv_w=lru_conv_w, lru_conv_b=lru_conv_b, lru_wa=lru_wa, lru_ba=lru_ba, lru_wx=lru_wx,
             lru_bx=lru_bx, lru_lambda=lru_lambda, norm_mlp_w=norm_mlp_w, norm_f_w=norm_f_w,
             w_in_p=[_pack_w_in(w_in[li]) for li in range(DEPTH)],
             w_out_p=[_pack_w_out(w_out[li]) for li in range(DEPTH)],
             w_up_b=[w_up[li].astype(BF16) for li in range(DEPTH)],
             w_down_b=[w_down[li].astype(BF16) for li in range(DEPTH)])
    y_prompt, st_p = _trunk(x_prompt, P, None)
    sample_state = (cache_swa_k, cache_swa_v, state_ssm, state_ssm_conv, state_hgrn, state_lru, state_lru_conv)
    y_sample, st_s = _trunk(x_sample, P, sample_state)
    return (y_prompt, y_sample, *st_p, *st_s)
```
